```python
import jax, jax.numpy as jnp
from jax import lax
import numpy as np

D_MODEL = 2048
BATCH = 4
SEQ = 4096
DEPTH = 4

N_MIXERS = 3
N_CONV_LAYERS = len(range(0, DEPTH, N_MIXERS))
N_GLA_LAYERS = len(range(1, DEPTH, N_MIXERS))
N_RET_LAYERS = len(range(2, DEPTH, N_MIXERS))
NORM_EPS = 1e-6
D_FF = -(-8 * D_MODEL // (3 * 256)) * 256
CONV_WIDTH = 3
GLA_HEADS = 4
GLA_KEY_DIM = D_MODEL // 2
GLA_DK = GLA_KEY_DIM // GLA_HEADS
GLA_DV = D_MODEL // GLA_HEADS
GLA_GATE_RANK = 16
GLA_GATE_TAU = 16.0
GLA_CHUNK = 64
GLA_IN_WIDTH = 2 * GLA_KEY_DIM + 2 * D_MODEL + GLA_GATE_RANK
RET_HEADS = 8
RET_DK = D_MODEL // RET_HEADS
RET_VALUE_FACTOR = 2
RET_V_WIDTH = RET_VALUE_FACTOR * D_MODEL
RET_DV = RET_V_WIDTH // RET_HEADS
RET_CHUNK = 128
RET_IN_WIDTH = 2 * D_MODEL + 2 * RET_V_WIDTH
ROPE_BASE = 10000.0

kernel_name = "hybrid_conv_gla_retention_trunk"


def _rmsnorm(x, g):
    xf = x.astype(jnp.float32)
    y = xf * lax.rsqrt(jnp.mean(xf * xf, axis=-1, keepdims=True) + NORM_EPS)
    return (y * g.astype(jnp.float32)).astype(x.dtype)


def _head_rmsnorm(o):
    return o * lax.rsqrt(jnp.mean(o * o, axis=-1, keepdims=True) + NORM_EPS)


def _to_chunks(t, chunk):
    b, s, h, d = t.shape
    return t.reshape(b, s // chunk, chunk, h, d).transpose(0, 3, 1, 2, 4)


def _from_chunks(o):
    b, h, n, c, d = o.shape
    return o.transpose(0, 2, 3, 1, 4).reshape(b, n * c, h * d)


def _inter_chunk(q_in, k_in, v, decay):
    b, h, _, _, dk = q_in.shape
    dv = v.shape[-1]

    def step(state, xs):
        qc, kc, vc, dc = xs
        out = jnp.einsum('bhcd,bhde->bhce', qc, state)
        state = state * dc[..., None] + jnp.einsum('bhcd,bhce->bhde', kc, vc)
        return state, out

    state0 = jnp.zeros((b, h, dk, dv), jnp.float32)
    xs = tuple(jnp.moveaxis(a, 2, 0) for a in (q_in, k_in, v, decay))
    _, out = lax.scan(step, state0, xs)
    return jnp.moveaxis(out, 0, 2)


def _short_conv_mixer(h, w_in, w_conv, w_out):
    proj = h @ w_in
    gate_b, gate_c, u = jnp.split(proj, 3, axis=-1)
    u = gate_c * u
    conv = lax.conv_general_dilated(
        u, w_conv[:, None, :].astype(u.dtype), window_strides=(1,),
        padding=[(CONV_WIDTH - 1, 0)], dimension_numbers=('NWC', 'WIO', 'NWC'),
        feature_group_count=D_MODEL)
    return (gate_b * conv) @ w_out


def _gla_mixer(h, w_in, w_gate2, b_gate, g_norm, w_out):
    b, s, _ = h.shape
    proj = h @ w_in
    q, k, v, r, z = jnp.split(
        proj, [GLA_KEY_DIM, 2 * GLA_KEY_DIM, 2 * GLA_KEY_DIM + D_MODEL,
               2 * GLA_KEY_DIM + 2 * D_MODEL], axis=-1)
    log_a = jax.nn.log_sigmoid((z @ w_gate2 + b_gate).astype(jnp.float32)) / GLA_GATE_TAU
    f32 = jnp.float32
    q = _to_chunks(q.astype(f32).reshape(b, s, GLA_HEADS, GLA_DK), GLA_CHUNK) * GLA_DK ** -0.5
    k = _to_chunks(k.astype(f32).reshape(b, s, GLA_HEADS, GLA_DK), GLA_CHUNK)
    v = _to_chunks(v.astype(f32).reshape(b, s, GLA_HEADS, GLA_DV), GLA_CHUNK)
    g = _to_chunks(log_a.reshape(b, s, GLA_HEADS, GLA_DK), GLA_CHUNK)
    cum = jnp.cumsum(g, axis=-2)
    cum_last = cum[..., -1:, :]
    q_t = q * jnp.exp(cum)
    k_t = k * jnp.exp(-cum)
    k_end = k * jnp.exp(cum_last - cum)
    causal = jnp.tril(jnp.ones((GLA_CHUNK, GLA_CHUNK), bool))
    att = jnp.where(causal, jnp.einsum('bhncd,bhnsd->bhncs', q_t, k_t), 0.0)
    o = jnp.einsum('bhncs,bhnse->bhnce', att, v)
    o = o + _inter_chunk(q_t, k_end, v, jnp.exp(cum_last[..., 0, :]))
    o = _head_rmsnorm(o) * g_norm.astype(f32)
    o = _from_chunks(o).astype(h.dtype)
    return (jax.nn.silu(r) * o) @ w_out


def _rotary(t, cos, sin):
    half = t.shape[-1] // 2
    rot = jnp.concatenate([-t[..., half:], t[..., :half]], axis=-1)
    return t * cos[:, None, :] + rot * sin[:, None, :]


def _retention_mixer(h, w_in, w_out):
    b, s, _ = h.shape
    f32 = jnp.float32
    proj = h @ w_in
    q, k, v, g = jnp.split(proj, [D_MODEL, 2 * D_MODEL, 2 * D_MODEL + RET_V_WIDTH], axis=-1)
    pos = jnp.arange(s, dtype=f32)
    inv_freq = 1.0 / (ROPE_BASE ** jnp.linspace(0.0, 1.0, RET_DK // 2, dtype=f32))
    ang = pos[:, None] * inv_freq[None, :]
    cos = jnp.cos(jnp.concatenate([ang, ang], axis=-1))
    sin = jnp.sin(jnp.concatenate([ang, ang], axis=-1))
    q = _to_chunks(_rotary(q.astype(f32).reshape(b, s, RET_HEADS, RET_DK), cos, sin), RET_CHUNK)
    k = _to_chunks(_rotary(k.astype(f32).reshape(b, s, RET_HEADS, RET_DK), cos, sin), RET_CHUNK) * RET_DK ** -0.5
    v = _to_chunks(v.astype(f32).reshape(b, s, RET_HEADS, RET_DV), RET_CHUNK)
    log_gamma = jnp.log(1.0 - 2.0 ** (-5.0 - jnp.arange(RET_HEADS, dtype=f32)))
    idx = jnp.arange(RET_CHUNK, dtype=f32)
    rel = idx[:, None] - idx[None, :]
    dmat = jnp.where(rel >= 0, jnp.exp(jnp.maximum(rel, 0.0)[None] * log_gamma[:, None, None]), 0.0)
    scores = jnp.einsum('bhncd,bhnsd->bhncs', q, k) * dmat[None, :, None]
    o = jnp.einsum('bhncs,bhnse->bhnce', scores, v)
    lg = log_gamma[None, :, None, None, None]
    q_in = q * jnp.exp(lg * (idx + 1.0)[:, None])
    k_in = k * jnp.exp(lg * (RET_CHUNK - 1.0 - idx)[:, None])
    n = q.shape[2]
    decay = jnp.broadcast_to(jnp.exp(log_gamma * RET_CHUNK)[None, :, None, None], (b, RET_HEADS, n, 1))
    o = o + _inter_chunk(q_in, k_in, v, decay)
    o = _from_chunks(_head_rmsnorm(o)).astype(h.dtype)
    return (jax.nn.silu(g) * o) @ w_out


def _swiglu(h, w_in, w_out):
    gate, up = jnp.split(h @ w_in, 2, axis=-1)
    return (jax.nn.silu(gate) * up) @ w_out


def setup_inputs(seed: int = 0) -> dict:
    key = jax.random.key(seed)
    ks = jax.random.split(key, 17)
    f32 = jnp.float32
    nrm = lambda k, shape, scale: jax.random.normal(k, shape, f32) * scale
    D = D_MODEL
    return {
        "x": nrm(ks[0], (BATCH, SEQ, D), 1.0),
        "norm_mix": 1.0 + nrm(ks[1], (DEPTH, D), 0.02),
        "norm_ffn": 1.0 + nrm(ks[2], (DEPTH, D), 0.02),
        "norm_final": 1.0 + nrm(ks[3], (D,), 0.02),
        "conv_w_in": nrm(ks[4], (N_CONV_LAYERS, D, 3 * D), D ** -0.5),
        "conv_w": nrm(ks[5], (N_CONV_LAYERS, CONV_WIDTH, D), CONV_WIDTH ** -0.5),
        "conv_w_out": nrm(ks[6], (N_CONV_LAYERS, D, D), D ** -0.5),
        "gla_w_in": nrm(ks[7], (N_GLA_LAYERS, D, GLA_IN_WIDTH), D ** -0.5),
        "gla_w_gate2": nrm(ks[8], (N_GLA_LAYERS, GLA_GATE_RANK, GLA_KEY_DIM), GLA_GATE_RANK ** -0.5),
        "gla_b_gate": nrm(ks[9], (N_GLA_LAYERS, GLA_KEY_DIM), 0.1),
        "gla_norm": 1.0 + nrm(ks[10], (N_GLA_LAYERS, GLA_DV), 0.02),
        "gla_w_out": nrm(ks[11], (N_GLA_LAYERS, D, D), D ** -0.5),
        "ret_w_in": nrm(ks[12], (N_RET_LAYERS, D, RET_IN_WIDTH), D ** -0.5),
        "ret_w_out": nrm(ks[13], (N_RET_LAYERS, RET_V_WIDTH, D), RET_V_WIDTH ** -0.5),
        "ffn_w_in": nrm(ks[14], (DEPTH, D, 2 * D_FF), D ** -0.5),
        "ffn_w_out": nrm(ks[15], (DEPTH, D_FF, D), D_FF ** -0.5),
    }


def reference(x, norm_mix, norm_ffn, norm_final, conv_w_in, conv_w, conv_w_out,
              gla_w_in, gla_w_gate2, gla_b_gate, gla_norm, gla_w_out,
              ret_w_in, ret_w_out, ffn_w_in, ffn_w_out):
    for i in range(DEPTH):
        h = _rmsnorm(x, norm_mix[i])
        kind, j = i % N_MIXERS, i // N_MIXERS
        if kind == 0:
            mix = _short_conv_mixer(h, conv_w_in[j], conv_w[j], conv_w_out[j])
        elif kind == 1:
            mix = _gla_mixer(h, gla_w_in[j], gla_w_gate2[j], gla_b_gate[j], gla_norm[j], gla_w_out[j])
        else:
            mix = _retention_mixer(h, ret_w_in[j], ret_w_out[j])
        x = x + mix
        x = x + _swiglu(_rmsnorm(x, norm_ffn[i]), ffn_w_in[i], ffn_w_out[i])
    return _rmsnorm(x, norm_final)
```

```python
import functools

import jax
import jax.numpy as jnp
from jax import lax
from jax.experimental import pallas as pl
from jax.experimental.pallas import tpu as pltpu

F32 = jnp.float32
BF16 = jnp.bfloat16

D_MODEL = 2048
NORM_EPS = 1e-6
CONV_WIDTH = 3
GLA_HEADS = 4
GLA_DK = 256
GLA_DV = 512
GLA_KEY_DIM = GLA_HEADS * GLA_DK
GLA_GATE_RANK = 16
GLA_GATE_TAU = 16.0
GLA_CHUNK = 64
RET_HEADS = 8
RET_DK = 256
RET_DV = 512
RET_V_WIDTH = RET_HEADS * RET_DV
RET_CHUNK = 128
ROPE_BASE = 10000.0

LANES = 128
SUBLANES = 8
VMEM_LIMIT = 56 * 1024 * 1024


def _params(semantics):
    return pltpu.CompilerParams(dimension_semantics=semantics,
                                vmem_limit_bytes=VMEM_LIMIT)


def _silu(v):
    return v * (1.0 / (1.0 + jnp.exp(-v)))


def _rms_rows(x, eps=NORM_EPS):
    return x * lax.rsqrt(jnp.mean(x * x, axis=-1, keepdims=True) + eps)


def _norm_rows_to_scratch(x_ref, g_ref, h_ref):
    @pl.when(pl.program_id(1) == 0)
    def _():
        h_ref[...] = (_rms_rows(x_ref[...]) * g_ref[...]).astype(BF16)


def _norm_proj_kernel(x_ref, g_ref, w_ref, o_ref, h_ref):
    _norm_rows_to_scratch(x_ref, g_ref, h_ref)
    o_ref[...] = jnp.dot(h_ref[...], w_ref[...],
                         preferred_element_type=F32).astype(o_ref.dtype)


def _norm_proj_z_kernel(x_ref, g_ref, w_ref, wz_ref, o_ref, z_ref, h_ref):
    _norm_rows_to_scratch(x_ref, g_ref, h_ref)

    @pl.when(pl.program_id(1) == 0)
    def _():
        z_ref[...] = jnp.dot(h_ref[...], wz_ref[...], preferred_element_type=F32)

    o_ref[...] = jnp.dot(h_ref[...], w_ref[...],
                         preferred_element_type=F32).astype(o_ref.dtype)


def _norm_swiglu_kernel(x_ref, g_ref, wg_ref, wu_ref, o_ref, h_ref):
    _norm_rows_to_scratch(x_ref, g_ref, h_ref)
    h = h_ref[...]
    gate = jnp.dot(h, wg_ref[...], preferred_element_type=F32)
    up = jnp.dot(h, wu_ref[...], preferred_element_type=F32)
    o_ref[...] = (_silu(gate) * up).astype(o_ref.dtype)


def _norm_conv_kernel(x_ref, g_ref, wb_ref, wc_ref, wu_ref, cw_ref, o_ref,
                      h_ref, u_ref, carry_ref, *, blocks_per_seq):
    m = pl.program_id(0)
    n = pl.program_id(1)
    tm = o_ref.shape[0]
    _norm_rows_to_scratch(x_ref, g_ref, h_ref)
    h = h_ref[...]
    u = (jnp.dot(h, wc_ref[...], preferred_element_type=F32)
         * jnp.dot(h, wu_ref[...], preferred_element_type=F32))
    first = (m % blocks_per_seq) == 0
    prev_tail = carry_ref[n]
    u_ref[0:SUBLANES, :] = jnp.where(first, jnp.zeros_like(prev_tail), prev_tail)
    u_ref[SUBLANES:SUBLANES + tm, :] = u
    carry_ref[n] = u[tm - SUBLANES:, :]
    cw = cw_ref[...]
    conv = (cw[0:1, :] * u_ref[SUBLANES - 2:SUBLANES - 2 + tm, :]
            + cw[1:2, :] * u_ref[SUBLANES - 1:SUBLANES - 1 + tm, :]
            + cw[2:3, :] * u)
    gate_b = jnp.dot(h, wb_ref[...], preferred_element_type=F32)
    o_ref[...] = (gate_b * conv).astype(o_ref.dtype)


def _row_spec(tm, width):
    return pl.BlockSpec((tm, width), lambda m, n: (m, 0))


def _col_spec(k, tn, offset_blocks=0):
    return pl.BlockSpec((k, tn), lambda m, n: (0, n + offset_blocks))


def _norm_proj(x, g, w, *, tm, tn, out_dtype=BF16):
    t, d = x.shape
    n_out = w.shape[1]
    return pl.pallas_call(
        _norm_proj_kernel,
        grid=(t // tm, n_out // tn),
        in_specs=[_row_spec(tm, d), pl.BlockSpec((1, d), lambda m, n: (0, 0)),
                  _col_spec(d, tn)],
        out_specs=pl.BlockSpec((tm, tn), lambda m, n: (m, n)),
        out_shape=jax.ShapeDtypeStruct((t, n_out), out_dtype),
        scratch_shapes=[pltpu.VMEM((tm, d), BF16)],
        compiler_params=_params(("parallel", "arbitrary")),
        name="norm_proj",
    )(x, g, w)


def _norm_proj_z(x, g, w, wz, *, tm, tn):
    t, d = x.shape
    n_out = w.shape[1]
    zw = wz.shape[1]
    return pl.pallas_call(
        _norm_proj_z_kernel,
        grid=(t // tm, n_out // tn),
        in_specs=[_row_spec(tm, d), pl.BlockSpec((1, d), lambda m, n: (0, 0)),
                  _col_spec(d, tn), pl.BlockSpec((d, zw), lambda m, n: (0, 0))],
        out_specs=[pl.BlockSpec((tm, tn), lambda m, n: (m, n)),
                   pl.BlockSpec((tm, zw), lambda m, n: (m, 0))],
        out_shape=[jax.ShapeDtypeStruct((t, n_out), BF16),
                   jax.ShapeDtypeStruct((t, zw), F32)],
        scratch_shapes=[pltpu.VMEM((tm, d), BF16)],
        compiler_params=_params(("parallel", "arbitrary")),
        name="norm_proj_z",
    )(x, g, w, wz)


def _norm_swiglu(x, g, w_in, *, tm, tn):
    t, d = x.shape
    d_ff = w_in.shape[1] // 2
    return pl.pallas_call(
        _norm_swiglu_kernel,
        grid=(t // tm, d_ff // tn),
        in_specs=[_row_spec(tm, d), pl.BlockSpec((1, d), lambda m, n: (0, 0)),
                  _col_spec(d, tn), _col_spec(d, tn, d_ff // tn)],
        out_specs=pl.BlockSpec((tm, tn), lambda m, n: (m, n)),
        out_shape=jax.ShapeDtypeStruct((t, d_ff), BF16),
        scratch_shapes=[pltpu.VMEM((tm, d), BF16)],
        compiler_params=_params(("parallel", "arbitrary")),
        name="norm_swiglu",
    )(x, g, w_in, w_in)


def _norm_conv(x, g, w_in, conv_w, *, seq, tm, tn):
    t, d = x.shape
    nb = d // tn
    kern = functools.partial(_norm_conv_kernel, blocks_per_seq=seq // tm)
    return pl.pallas_call(
        kern,
        grid=(t // tm, nb),
        in_specs=[_row_spec(tm, d), pl.BlockSpec((1, d), lambda m, n: (0, 0)),
                  _col_spec(d, tn), _col_spec(d, tn, nb), _col_spec(d, tn, 2 * nb),
                  pl.BlockSpec((CONV_WIDTH, tn), lambda m, n: (0, n))],
        out_specs=pl.BlockSpec((tm, tn), lambda m, n: (m, n)),
        out_shape=jax.ShapeDtypeStruct((t, d), BF16),
        scratch_shapes=[pltpu.VMEM((tm, d), BF16),
                        pltpu.VMEM((tm + SUBLANES, tn), F32),
                        pltpu.VMEM((nb, SUBLANES, tn), F32)],
        compiler_params=_params(("arbitrary", "arbitrary")),
        name="norm_conv",
    )(x, g, w_in, w_in, w_in, conv_w)


def _proj_residual_kernel(a_ref, w_ref, x_ref, o_ref):
    o_ref[...] = x_ref[...] + jnp.dot(a_ref[...], w_ref[...],
                                      preferred_element_type=F32)


def _proj_residual(a, w, x, *, tm, tn):
    t, k = a.shape
    d = w.shape[1]
    return pl.pallas_call(
        _proj_residual_kernel,
        grid=(d // tn, t // tm),
        in_specs=[pl.BlockSpec((tm, k), lambda n, m: (m, 0)),
                  pl.BlockSpec((k, tn), lambda n, m: (0, n)),
                  pl.BlockSpec((tm, tn), lambda n, m: (m, n))],
        out_specs=pl.BlockSpec((tm, tn), lambda n, m: (m, n)),
        out_shape=jax.ShapeDtypeStruct((t, d), F32),
        compiler_params=_params(("parallel", "parallel")),
        name="proj_residual",
    )(a, w, x)


def _final_norm_kernel(x_ref, g_ref, o_ref):
    o_ref[...] = _rms_rows(x_ref[...]) * g_ref[...]


def _final_norm(x, g, *, tm):
    t, d = x.shape
    return pl.pallas_call(
        _final_norm_kernel,
        grid=(t // tm,),
        in_specs=[pl.BlockSpec((tm, d), lambda m: (m, 0)),
                  pl.BlockSpec((1, d), lambda m: (0, 0))],
        out_specs=pl.BlockSpec((tm, d), lambda m: (m, 0)),
        out_shape=jax.ShapeDtypeStruct((t, d), F32),
        compiler_params=_params(("parallel",)),
        name="final_norm",
    )(x, g)


def _gla_kernel(q_ref, k_ref, v_ref, r_ref, z_ref, wg_ref, bg_ref, gn_ref,
                o_ref, st_ref, *, chunks):
    c = GLA_CHUNK

    @pl.when(pl.program_id(2) == 0)
    def _():
        st_ref[...] = jnp.zeros_like(st_ref)

    row = lax.broadcasted_iota(jnp.int32, (c, c), 0)
    col = lax.broadcasted_iota(jnp.int32, (c, c), 1)
    causal = row >= col
    tril = causal.astype(F32)
    wg = wg_ref[...]
    bg = bg_ref[...]
    gn = gn_ref[...]

    def body(i, carry):
        rows = pl.ds(pl.multiple_of(i * c, c), c)
        q = q_ref[rows, :].astype(F32) * (GLA_DK ** -0.5)
        k = k_ref[rows, :].astype(F32)
        v = v_ref[rows, :]
        pre = jnp.dot(z_ref[rows, :], wg, preferred_element_type=F32,
                      precision=lax.Precision.HIGHEST) + bg
        log_a = (jnp.minimum(pre, 0.0)
                 - jnp.log1p(jnp.exp(-jnp.abs(pre)))) / GLA_GATE_TAU
        cum = jnp.dot(tril, log_a, preferred_element_type=F32,
                      precision=lax.Precision.HIGHEST)
        cum_last = cum[c - 1:c, :]
        q_t = (q * jnp.exp(cum)).astype(BF16)
        k_t = (k * jnp.exp(-cum)).astype(BF16)
        k_end = (k * jnp.exp(cum_last - cum)).astype(BF16)
        att = lax.dot_general(q_t, k_t, (((1,), (1,)), ((), ())),
                              preferred_element_type=F32)
        att = jnp.where(causal, att, 0.0).astype(BF16)
        st = st_ref[...]
        o = jnp.dot(att, v, preferred_element_type=F32)
        o = o + lax.dot_general(q_t, st.astype(BF16), (((1,), (1,)), ((), ())),
                                preferred_element_type=F32)
        st_ref[...] = st * jnp.exp(cum_last) + lax.dot_general(
            v, k_end, (((0,), (0,)), ((), ())), preferred_element_type=F32)
        o = _rms_rows(o) * gn
        o_ref[rows, :] = (_silu(r_ref[rows, :].astype(F32)) * o).astype(o_ref.dtype)
        return carry

    lax.fori_loop(0, chunks, body, 0)


def _gla_core(proj, z, w_gate2, b_gate, g_norm, *, batch, seq, tr):
    t = proj.shape[0]
    nblk = seq // tr
    zw = z.shape[1]
    k_off = GLA_KEY_DIM // GLA_DK
    v_off = 2 * GLA_KEY_DIM // GLA_DV
    r_off = v_off + D_MODEL // GLA_DV
    rows = lambda b, h, s: b * nblk + s
    kern = functools.partial(_gla_kernel, chunks=tr // GLA_CHUNK)
    return pl.pallas_call(
        kern,
        grid=(batch, GLA_HEADS, nblk),
        in_specs=[
            pl.BlockSpec((tr, GLA_DK), lambda b, h, s: (rows(b, h, s), h)),
            pl.BlockSpec((tr, GLA_DK), lambda b, h, s: (rows(b, h, s), k_off + h)),
            pl.BlockSpec((tr, GLA_DV), lambda b, h, s: (rows(b, h, s), v_off + h)),
            pl.BlockSpec((tr, GLA_DV), lambda b, h, s: (rows(b, h, s), r_off + h)),
            pl.BlockSpec((tr, zw), lambda b, h, s: (rows(b, h, s), 0)),
            pl.BlockSpec((zw, GLA_DK), lambda b, h, s: (0, h)),
            pl.BlockSpec((1, GLA_DK), lambda b, h, s: (0, h)),
            pl.BlockSpec((1, GLA_DV), lambda b, h, s: (0, 0)),
        ],
        out_specs=pl.BlockSpec((tr, GLA_DV), lambda b, h, s: (rows(b, h, s), h)),
        out_shape=jax.ShapeDtypeStruct((t, D_MODEL), BF16),
        scratch_shapes=[pltpu.VMEM((GLA_DV, GLA_DK), F32)],
        compiler_params=_params(("parallel", "parallel", "arbitrary")),
        name="gla_core",
    )(proj, proj, proj, proj, z, w_gate2, b_gate, g_norm)


def _ret_kernel(q_ref, k_ref, v_ref, g_ref, cos_ref, sin_ref, dmat_ref,
                qf_ref, kf_ref, dec_ref, o_ref, s_ref, *, chunks):
    c = RET_CHUNK

    @pl.when(pl.program_id(2) == 0)
    def _():
        s_ref[...] = jnp.zeros_like(s_ref)

    dmat = dmat_ref[0]
    qf = qf_ref[0]
    kf = kf_ref[0]
    dec = dec_ref[0]

    def rotary(t, cos, sin_signed):
        return t * cos + pltpu.roll(t, RET_DK // 2, axis=1) * sin_signed

    def body(i, carry):
        rows = pl.ds(pl.multiple_of(i * c, c), c)
        cos = cos_ref[rows, :]
        sin = sin_ref[rows, :]
        q = rotary(q_ref[rows, :].astype(F32), cos, sin)
        k = rotary(k_ref[rows, :].astype(F32), cos, sin) * (RET_DK ** -0.5)
        v = v_ref[rows, :]
        scores = lax.dot_general(q.astype(BF16), k.astype(BF16),
                                 (((1,), (1,)), ((), ())),
                                 preferred_element_type=F32) * dmat
        s = s_ref[...]
        o = jnp.dot(scores.astype(BF16), v, preferred_element_type=F32)
        o = o + jnp.dot((q * qf).astype(BF16), s.astype(BF16),
                        preferred_element_type=F32)
        s_ref[...] = s * dec + lax.dot_general(
            (k * kf).astype(BF16), v, (((0,), (0,)), ((), ())),
            preferred_element_type=F32)
        o = _rms_rows(o)
        o_ref[rows, :] = (_silu(g_ref[rows, :].astype(F32)) * o).astype(o_ref.dtype)
        return carry

    lax.fori_loop(0, chunks, body, 0)


def _ret_tables(seq):
    pos = jnp.arange(seq, dtype=F32)
    inv_freq = 1.0 / (ROPE_BASE ** jnp.linspace(0.0, 1.0, RET_DK // 2, dtype=F32))
    ang = pos[:, None] * inv_freq[None, :]
    cos = jnp.cos(jnp.concatenate([ang, ang], axis=-1))
    sin = jnp.sin(ang)
    sin_signed = jnp.concatenate([-sin, sin], axis=-1)
    log_gamma = jnp.log(1.0 - 2.0 ** (-5.0 - jnp.arange(RET_HEADS, dtype=F32)))
    idx = jnp.arange(RET_CHUNK, dtype=F32)
    rel = idx[:, None] - idx[None, :]
    dmat = jnp.where(rel >= 0,
                     jnp.exp(jnp.maximum(rel, 0.0)[None] * log_gamma[:, None, None]), 0.0)
    ones = jnp.ones((1, 1, RET_DK), F32)
    qfac = jnp.exp(log_gamma[:, None, None] * (idx + 1.0)[None, :, None]) * ones
    kfac = jnp.exp(log_gamma[:, None, None] * (RET_CHUNK - 1.0 - idx)[None, :, None]) * ones
    decay = jnp.exp(log_gamma * RET_CHUNK)[:, None, None] * jnp.ones((1, 1, RET_DV), F32)
    return cos, sin_signed, dmat, qfac, kfac, decay


def _ret_core(proj, *, batch, seq, tr):
    t = proj.shape[0]
    nblk = seq // tr
    cos, sin_signed, dmat, qfac, kfac, decay = _ret_tables(seq)
    k_off = D_MODEL // RET_DK
    v_off = 2 * D_MODEL // RET_DV
    g_off = v_off + RET_V_WIDTH // RET_DV
    rows = lambda b, h, s: b * nblk + s
    kern = functools.partial(_ret_kernel, chunks=tr // RET_CHUNK)
    head_tab = lambda r, w: pl.BlockSpec((1, r, w), lambda b, h, s: (h, 0, 0))
    return pl.pallas_call(
        kern,
        grid=(batch, RET_HEADS, nblk),
        in_specs=[
            pl.BlockSpec((tr, RET_DK), lambda b, h, s: (rows(b, h, s), h)),
            pl.BlockSpec((tr, RET_DK), lambda b, h, s: (rows(b, h, s), k_off + h)),
            pl.BlockSpec((tr, RET_DV), lambda b, h, s: (rows(b, h, s), v_off + h)),
            pl.BlockSpec((tr, RET_DV), lambda b, h, s: (rows(b, h, s), g_off + h)),
            pl.BlockSpec((tr, RET_DK), lambda b, h, s: (s, 0)),
            pl.BlockSpec((tr, RET_DK), lambda b, h, s: (s, 0)),
            head_tab(RET_CHUNK, RET_CHUNK),
            head_tab(RET_CHUNK, RET_DK),
            head_tab(RET_CHUNK, RET_DK),
            head_tab(1, RET_DV),
        ],
        out_specs=pl.BlockSpec((tr, RET_DV), lambda b, h, s: (rows(b, h, s), h)),
        out_shape=jax.ShapeDtypeStruct((t, RET_V_WIDTH), BF16),
        scratch_shapes=[pltpu.VMEM((RET_DK, RET_DV), F32)],
        compiler_params=_params(("parallel", "parallel", "arbitrary")),
        name="ret_core",
    )(proj, proj, proj, proj, cos, sin_signed, dmat, qfac, kfac, decay)


def kernel(x, norm_mix, norm_ffn, norm_final, conv_w_in, conv_w, conv_w_out,
           gla_w_in, gla_w_gate2, gla_b_gate, gla_norm, gla_w_out,
           ret_w_in, ret_w_out, ffn_w_in, ffn_w_out):
    batch, seq, d = x.shape
    depth = norm_mix.shape[0]
    xt = x.reshape(batch * seq, d)
    bf = lambda w: w.astype(BF16)
    gla_main = 2 * GLA_KEY_DIM + 2 * D_MODEL

    for i in range(depth):
        kind, j = i % 3, i // 3
        g_mix = norm_mix[i][None, :]
        if kind == 0:
            a = _norm_conv(xt, g_mix, bf(conv_w_in[j]), conv_w[j],
                           seq=seq, tm=1024, tn=512)
            xt = _proj_residual(a, bf(conv_w_out[j]), xt, tm=1024, tn=1024)
        elif kind == 1:
            w = gla_w_in[j]
            wz = jnp.pad(w[:, gla_main:], ((0, 0), (0, LANES - GLA_GATE_RANK)))
            proj, z = _norm_proj_z(xt, g_mix, bf(w[:, :gla_main]), bf(wz),
                                   tm=1024, tn=1024)
            wg2 = jnp.pad(gla_w_gate2[j], ((0, LANES - GLA_GATE_RANK), (0, 0)))
            a = _gla_core(proj, z, wg2, gla_b_gate[j][None, :],
                          gla_norm[j][None, :], batch=batch, seq=seq, tr=512)
            xt = _proj_residual(a, bf(gla_w_out[j]), xt, tm=1024, tn=1024)
        else:
            proj = _norm_proj(xt, g_mix, bf(ret_w_in[j]), tm=1024, tn=1024)
            a = _ret_core(proj, batch=batch, seq=seq, tr=512)
            xt = _proj_residual(a, bf(ret_w_out[j]), xt, tm=1024, tn=1024)
        a = _norm_swiglu(xt, norm_ffn[i][None, :], bf(ffn_w_in[i]), tm=1024, tn=512)
        xt = _proj_residual(a, bf(ffn_w_out[i]), xt, tm=512, tn=1024)

    out = _final_norm(xt, norm_final[None, :], tm=512)
    return out.reshape(batch, seq, d)
```

```python
import functools

import jax
import jax.numpy as jnp
from jax import lax
from jax.experimental import pallas as pl
from jax.experimental.pallas import tpu as pltpu

F32 = jnp.float32
BF16 = jnp.bfloat16

D_MODEL = 2048
NORM_EPS = 1e-6
CONV_WIDTH = 3
GLA_HEADS = 4
GLA_DK = 256
GLA_DV = 512
GLA_KEY_DIM = GLA_HEADS * GLA_DK
GLA_GATE_RANK = 16
GLA_GATE_TAU = 16.0
GLA_CHUNK = 64
RET_HEADS = 8
RET_DK = 256
RET_DV = 512
RET_V_WIDTH = RET_HEADS * RET_DV
RET_CHUNK = 128
ROPE_BASE = 10000.0

LANES = 128
SUBLANES = 8
VMEM_LIMIT = 56 * 1024 * 1024

NT_DIMS = (((1,), (1,)), ((), ()))
TN_DIMS = (((0,), (0,)), ((), ()))


def _params(semantics):
    return pltpu.CompilerParams(dimension_semantics=semantics,
                                vmem_limit_bytes=VMEM_LIMIT)


def _silu(v):
    return v * (1.0 / (1.0 + jnp.exp(-v)))


def _rms_rows(x, eps=NORM_EPS):
    return x * lax.rsqrt(jnp.mean(x * x, axis=-1, keepdims=True) + eps)


def _dot(a, b):
    return jnp.dot(a, b, preferred_element_type=F32)


def _norm_kernel(x_ref, g_ref, o_ref):
    o_ref[...] = (_rms_rows(x_ref[...]) * g_ref[...]).astype(o_ref.dtype)


def _norm(x, g, *, tm, out_dtype):
    t, d = x.shape
    return pl.pallas_call(
        _norm_kernel,
        grid=(t // tm,),
        in_specs=[pl.BlockSpec((tm, d), lambda m: (m, 0)),
                  pl.BlockSpec((1, d), lambda m: (0, 0))],
        out_specs=pl.BlockSpec((tm, d), lambda m: (m, 0)),
        out_shape=jax.ShapeDtypeStruct((t, d), out_dtype),
        compiler_params=_params(("parallel",)),
        name="norm",
    )(x, g)


def _proj_kernel(h_ref, w_ref, o_ref):
    o_ref[...] = _dot(h_ref[...], w_ref[...].astype(BF16)).astype(o_ref.dtype)


def _proj_z_kernel(h_ref, w_ref, wz_ref, o_ref, z_ref):
    @pl.when(pl.program_id(1) == 0)
    def _():
        z_ref[...] = _dot(h_ref[...], wz_ref[...].astype(BF16))

    o_ref[...] = _dot(h_ref[...], w_ref[...].astype(BF16)).astype(o_ref.dtype)


def _swiglu_kernel(h_ref, wg_ref, wu_ref, o_ref):
    h = h_ref[...]
    gate = _dot(h, wg_ref[...].astype(BF16))
    up = _dot(h, wu_ref[...].astype(BF16))
    o_ref[...] = (_silu(gate) * up).astype(o_ref.dtype)


def _conv_kernel(h_ref, wb_ref, wc_ref, wu_ref, cw_ref, o_ref,
                 u_ref, carry_ref, *, blocks_per_seq):
    m = pl.program_id(0)
    n = pl.program_id(1)
    tm = o_ref.shape[0]
    h = h_ref[...]
    u = _dot(h, wc_ref[...].astype(BF16)) * _dot(h, wu_ref[...].astype(BF16))
    first = (m % blocks_per_seq) == 0
    prev_tail = carry_ref[n]
    u_ref[0:SUBLANES, :] = jnp.where(first, jnp.zeros_like(prev_tail), prev_tail)
    u_ref[SUBLANES:SUBLANES + tm, :] = u
    carry_ref[n] = u[tm - SUBLANES:, :]
    cw = cw_ref[...]
    conv = (cw[0:1, :] * u_ref[SUBLANES - 2:SUBLANES - 2 + tm, :]
            + cw[1:2, :] * u_ref[SUBLANES - 1:SUBLANES - 1 + tm, :]
            + cw[2:3, :] * u)
    gate_b = _dot(h, wb_ref[...].astype(BF16))
    o_ref[...] = (gate_b * conv).astype(o_ref.dtype)


def _row_spec(tm, width):
    return pl.BlockSpec((tm, width), lambda m, n: (m, 0))


def _col_spec(layer, k, tn, offset_blocks=0):
    return pl.BlockSpec((None, k, tn), lambda m, n: (layer, 0, n + offset_blocks))


def _proj(h, w, layer, *, n_out, tm, tn):
    t, d = h.shape
    return pl.pallas_call(
        _proj_kernel,
        grid=(t // tm, n_out // tn),
        in_specs=[_row_spec(tm, d), _col_spec(layer, d, tn)],
        out_specs=pl.BlockSpec((tm, tn), lambda m, n: (m, n)),
        out_shape=jax.ShapeDtypeStruct((t, n_out), BF16),
        compiler_params=_params(("parallel", "arbitrary")),
        name="proj",
    )(h, w)


def _proj_z(h, w, layer, wz, *, n_out, tm, tn):
    t, d = h.shape
    zw = wz.shape[1]
    return pl.pallas_call(
        _proj_z_kernel,
        grid=(t // tm, n_out // tn),
        in_specs=[_row_spec(tm, d), _col_spec(layer, d, tn),
                  pl.BlockSpec((d, zw), lambda m, n: (0, 0))],
        out_specs=[pl.BlockSpec((tm, tn), lambda m, n: (m, n)),
                   pl.BlockSpec((tm, zw), lambda m, n: (m, 0))],
        out_shape=[jax.ShapeDtypeStruct((t, n_out), BF16),
                   jax.ShapeDtypeStruct((t, zw), F32)],
        compiler_params=_params(("parallel", "arbitrary")),
        name="proj_z",
    )(h, w, wz)


def _swiglu(h, w_in, layer, *, tm, tn):
    t, d = h.shape
    d_ff = w_in.shape[2] // 2
    return pl.pallas_call(
        _swiglu_kernel,
        grid=(t // tm, d_ff // tn),
        in_specs=[_row_spec(tm, d), _col_spec(layer, d, tn),
                  _col_spec(layer, d, tn, d_ff // tn)],
        out_specs=pl.BlockSpec((tm, tn), lambda m, n: (m, n)),
        out_shape=jax.ShapeDtypeStruct((t, d_ff), BF16),
        compiler_params=_params(("parallel", "arbitrary")),
        name="swiglu",
    )(h, w_in, w_in)


def _conv(h, w_in, conv_w, layer, *, seq, tm, tn):
    t, d = h.shape
    nb = d // tn
    kern = functools.partial(_conv_kernel, blocks_per_seq=seq // tm)
    return pl.pallas_call(
        kern,
        grid=(t // tm, nb),
        in_specs=[_row_spec(tm, d),
                  _col_spec(layer, d, tn), _col_spec(layer, d, tn, nb),
                  _col_spec(layer, d, tn, 2 * nb),
                  pl.BlockSpec((None, CONV_WIDTH, tn), lambda m, n: (layer, 0, n))],
        out_specs=pl.BlockSpec((tm, tn), lambda m, n: (m, n)),
        out_shape=jax.ShapeDtypeStruct((t, d), BF16),
        scratch_shapes=[pltpu.VMEM((tm + SUBLANES, tn), F32),
                        pltpu.VMEM((nb, SUBLANES, tn), F32)],
        compiler_params=_params(("arbitrary", "arbitrary")),
        name="conv",
    )(h, w_in, w_in, w_in, conv_w)


def _proj_residual_kernel(a_ref, w_ref, x_ref, g_ref, *out_refs, nk, final):
    part = _dot(a_ref[...], w_ref[...])

    def finish(xn):
        normed = _rms_rows(xn) * g_ref[...]
        if final:
            out_refs[0][...] = normed
        else:
            out_refs[0][...] = xn
            out_refs[1][...] = normed.astype(BF16)

    if nk == 1:
        finish(x_ref[...] + part)
        return

    acc_ref = out_refs[0]
    k = pl.program_id(1)

    @pl.when(k == 0)
    def _():
        acc_ref[...] = x_ref[...] + part

    if nk > 2:
        @pl.when(jnp.logical_and(k > 0, k < nk - 1))
        def _():
            acc_ref[...] += part

    @pl.when(k == nk - 1)
    def _():
        finish(acc_ref[...] + part)


def _proj_residual(a, w, x, g, *, tm, nk, final=False):
    t, kdim = a.shape
    d = w.shape[1]
    tk = kdim // nk
    kern = functools.partial(_proj_residual_kernel, nk=nk, final=final)
    row_out = pl.BlockSpec((tm, d), lambda m, k: (m, 0))
    if final:
        out_specs, out_shape = row_out, jax.ShapeDtypeStruct((t, d), F32)
    else:
        out_specs = [row_out, row_out]
        out_shape = [jax.ShapeDtypeStruct((t, d), F32),
                     jax.ShapeDtypeStruct((t, d), BF16)]
    return pl.pallas_call(
        kern,
        grid=(t // tm, nk),
        in_specs=[pl.BlockSpec((tm, tk), lambda m, k: (m, k)),
                  pl.BlockSpec((tk, d), lambda m, k: (k, 0)),
                  row_out,
                  pl.BlockSpec((1, d), lambda m, k: (0, 0))],
        out_specs=out_specs,
        out_shape=out_shape,
        compiler_params=_params(("parallel", "arbitrary")),
        name="proj_residual",
    )(a, w, x, g)


def _log_sigmoid(v):
    return jnp.minimum(v, 0.0) - jnp.log1p(jnp.exp(-jnp.abs(v)))


def _gla_kernel(q_ref, k_ref, v_ref, r_ref, z_ref, wg_ref, bg_ref, gn_ref,
                o_ref, st_ref, *, chunks):
    c = GLA_CHUNK

    @pl.when(pl.program_id(2) == 0)
    def _():
        st_ref[...] = jnp.zeros_like(st_ref)

    row = lax.broadcasted_iota(jnp.int32, (c, c), 0)
    col = lax.broadcasted_iota(jnp.int32, (c, c), 1)
    causal = row >= col
    tril = jnp.where(causal, 1.0, 0.0).astype(BF16)
    gn = gn_ref[...]

    pre = jnp.dot(z_ref[...], wg_ref[...], preferred_element_type=F32,
                  precision=lax.Precision.HIGHEST) + bg_ref[...]
    log_a = _log_sigmoid(pre) * (1.0 / GLA_GATE_TAU)
    a_hi = log_a.astype(BF16)
    a_lo = (log_a - a_hi.astype(F32)).astype(BF16)

    st = st_ref[...]
    for i in range(chunks):
        rows = slice(i * c, (i + 1) * c)
        cum = _dot(tril, a_hi[rows]) + _dot(tril, a_lo[rows])
        cum_last = cum[c - 1:c, :]
        q = q_ref[rows, :].astype(F32) * (GLA_DK ** -0.5)
        k = k_ref[rows, :].astype(F32)
        v = v_ref[rows, :]
        q_t = (q * jnp.exp(cum)).astype(BF16)
        k_t = (k * jnp.exp(-cum)).astype(BF16)
        k_end = (k * jnp.exp(cum_last - cum)).astype(BF16)
        att = lax.dot_general(q_t, k_t, NT_DIMS, preferred_element_type=F32)
        att = jnp.where(causal, att, 0.0).astype(BF16)
        o = _dot(att, v) + lax.dot_general(q_t, st.astype(BF16), NT_DIMS,
                                           preferred_element_type=F32)
        st = st * jnp.exp(cum_last) + lax.dot_general(
            v, k_end, TN_DIMS, preferred_element_type=F32)
        o = _rms_rows(o) * gn
        o_ref[rows, :] = (_silu(r_ref[rows, :].astype(F32)) * o).astype(o_ref.dtype)
    st_ref[...] = st


def _gla_core(proj, z, w_gate2, b_gate, g_norm, *, batch, seq, tr):
    t = proj.shape[0]
    nblk = seq // tr
    zw = z.shape[1]
    k_off = GLA_KEY_DIM // GLA_DK
    v_off = 2 * GLA_KEY_DIM // GLA_DV
    r_off = v_off + D_MODEL // GLA_DV
    rows = lambda b, h, s: b * nblk + s
    kern = functools.partial(_gla_kernel, chunks=tr // GLA_CHUNK)
    return pl.pallas_call(
        kern,
        grid=(batch, GLA_HEADS, nblk),
        in_specs=[
            pl.BlockSpec((tr, GLA_DK), lambda b, h, s: (rows(b, h, s), h)),
            pl.BlockSpec((tr, GLA_DK), lambda b, h, s: (rows(b, h, s), k_off + h)),
            pl.BlockSpec((tr, GLA_DV), lambda b, h, s: (rows(b, h, s), v_off + h)),
            pl.BlockSpec((tr, GLA_DV), lambda b, h, s: (rows(b, h, s), r_off + h)),
            pl.BlockSpec((tr, zw), lambda b, h, s: (rows(b, h, s), 0)),
            pl.BlockSpec((zw, GLA_DK), lambda b, h, s: (0, h)),
            pl.BlockSpec((1, GLA_DK), lambda b, h, s: (0, h)),
            pl.BlockSpec((1, GLA_DV), lambda b, h, s: (0, 0)),
        ],
        out_specs=pl.BlockSpec((tr, GLA_DV), lambda b, h, s: (rows(b, h, s), h)),
        out_shape=jax.ShapeDtypeStruct((t, D_MODEL), BF16),
        scratch_shapes=[pltpu.VMEM((GLA_DV, GLA_DK), F32)],
        compiler_params=_params(("parallel", "parallel", "arbitrary")),
        name="gla_core",
    )(proj, proj, proj, proj, z, w_gate2, b_gate, g_norm)


def _ret_kernel(q_ref, k_ref, v_ref, g_ref, cos_ref, sin_ref, dmat_ref,
                qf_ref, kf_ref, dec_ref, o_ref, s_ref, *, chunks):
    c = RET_CHUNK

    @pl.when(pl.program_id(2) == 0)
    def _():
        s_ref[...] = jnp.zeros_like(s_ref)

    dmat = dmat_ref[0]
    qf = qf_ref[0]
    kf = kf_ref[0]
    dec = dec_ref[0]

    def rotary(t, cos, sin_signed):
        return t * cos + pltpu.roll(t, RET_DK // 2, axis=1) * sin_signed

    s = s_ref[...]
    for i in range(chunks):
        rows = slice(i * c, (i + 1) * c)
        cos = cos_ref[rows, :]
        sin = sin_ref[rows, :]
        q = rotary(q_ref[rows, :].astype(F32), cos, sin)
        k = rotary(k_ref[rows, :].astype(F32), cos, sin) * (RET_DK ** -0.5)
        v = v_ref[rows, :]
        scores = lax.dot_general(q.astype(BF16), k.astype(BF16), NT_DIMS,
                                 preferred_element_type=F32) * dmat
        o = _dot(scores.astype(BF16), v) + _dot((q * qf).astype(BF16), s.astype(BF16))
        s = s * dec + lax.dot_general((k * kf).astype(BF16), v, TN_DIMS,
                                      preferred_element_type=F32)
        o = _rms_rows(o)
        o_ref[rows, :] = (_silu(g_ref[rows, :].astype(F32)) * o).astype(o_ref.dtype)
    s_ref[...] = s


def _ret_tables(seq):
    pos = jnp.arange(seq, dtype=F32)
    inv_freq = 1.0 / (ROPE_BASE ** jnp.linspace(0.0, 1.0, RET_DK // 2, dtype=F32))
    ang = pos[:, None] * inv_freq[None, :]
    cos = jnp.cos(jnp.concatenate([ang, ang], axis=-1))
    sin = jnp.sin(ang)
    sin_signed = jnp.concatenate([-sin, sin], axis=-1)
    log_gamma = jnp.log(1.0 - 2.0 ** (-5.0 - jnp.arange(RET_HEADS, dtype=F32)))
    idx = jnp.arange(RET_CHUNK, dtype=F32)
    rel = idx[:, None] - idx[None, :]
    dmat = jnp.where(rel >= 0,
                     jnp.exp(jnp.maximum(rel, 0.0)[None] * log_gamma[:, None, None]), 0.0)
    ones = jnp.ones((1, 1, RET_DK), F32)
    qfac = jnp.exp(log_gamma[:, None, None] * (idx + 1.0)[None, :, None]) * ones
    kfac = jnp.exp(log_gamma[:, None, None] * (RET_CHUNK - 1.0 - idx)[None, :, None]) * ones
    decay = jnp.exp(log_gamma * RET_CHUNK)[:, None, None] * jnp.ones((1, 1, RET_DV), F32)
    return cos, sin_signed, dmat, qfac, kfac, decay


def _ret_core(proj, *, batch, seq, tr):
    t = proj.shape[0]
    nblk = seq // tr
    cos, sin_signed, dmat, qfac, kfac, decay = _ret_tables(seq)
    k_off = D_MODEL // RET_DK
    v_off = 2 * D_MODEL // RET_DV
    g_off = v_off + RET_V_WIDTH // RET_DV
    rows = lambda b, h, s: b * nblk + s
    kern = functools.partial(_ret_kernel, chunks=tr // RET_CHUNK)
    head_tab = lambda r, w: pl.BlockSpec((1, r, w), lambda b, h, s: (h, 0, 0))
    return pl.pallas_call(
        kern,
        grid=(batch, RET_HEADS, nblk),
        in_specs=[
            pl.BlockSpec((tr, RET_DK), lambda b, h, s: (rows(b, h, s), h)),
            pl.BlockSpec((tr, RET_DK), lambda b, h, s: (rows(b, h, s), k_off + h)),
            pl.BlockSpec((tr, RET_DV), lambda b, h, s: (rows(b, h, s), v_off + h)),
            pl.BlockSpec((tr, RET_DV), lambda b, h, s: (rows(b, h, s), g_off + h)),
            pl.BlockSpec((tr, RET_DK), lambda b, h, s: (s, 0)),
            pl.BlockSpec((tr, RET_DK), lambda b, h, s: (s, 0)),
            head_tab(RET_CHUNK, RET_CHUNK),
            head_tab(RET_CHUNK, RET_DK),
            head_tab(RET_CHUNK, RET_DK),
            head_tab(1, RET_DV),
        ],
        out_specs=pl.BlockSpec((tr, RET_DV), lambda b, h, s: (rows(b, h, s), h)),
        out_shape=jax.ShapeDtypeStruct((t, RET_V_WIDTH), BF16),
        scratch_shapes=[pltpu.VMEM((RET_DK, RET_DV), F32)],
        compiler_params=_params(("parallel", "parallel", "arbitrary")),
        name="ret_core",
    )(proj, proj, proj, proj, cos, sin_signed, dmat, qfac, kfac, decay)


def kernel(x, norm_mix, norm_ffn, norm_final, conv_w_in, conv_w, conv_w_out,
           gla_w_in, gla_w_gate2, gla_b_gate, gla_norm, gla_w_out,
           ret_w_in, ret_w_out, ffn_w_in, ffn_w_out):
    batch, seq, d = x.shape
    depth = norm_mix.shape[0]
    xt = x.reshape(batch * seq, d)
    bf = lambda w: w.astype(BF16)
    gla_main = 2 * GLA_KEY_DIM + 2 * D_MODEL

    h = _norm(xt, norm_mix[0][None, :], tm=512, out_dtype=BF16)
    for i in range(depth):
        kind, j = i % 3, i // 3
        g_ffn = norm_ffn[i][None, :]
        if kind == 0:
            a = _conv(h, conv_w_in, conv_w, j, seq=seq, tm=2048, tn=256)
            xt, h = _proj_residual(a, bf(conv_w_out[j]), xt, g_ffn, tm=512, nk=1)
        elif kind == 1:
            wz = jnp.pad(gla_w_in[j, :, gla_main:],
                         ((0, 0), (0, LANES - GLA_GATE_RANK)))
            proj, z = _proj_z(h, gla_w_in, j, wz, n_out=gla_main, tm=2048, tn=512)
            wg2 = jnp.pad(gla_w_gate2[j], ((0, LANES - GLA_GATE_RANK), (0, 0)))
            a = _gla_core(proj, z, wg2, gla_b_gate[j][None, :],
                          gla_norm[j][None, :], batch=batch, seq=seq, tr=512)
            xt, h = _proj_residual(a, bf(gla_w_out[j]), xt, g_ffn, tm=512, nk=1)
        else:
            proj = _proj(h, ret_w_in, j, n_out=ret_w_in.shape[2], tm=2048, tn=512)
            a = _ret_core(proj, batch=batch, seq=seq, tr=512)
            xt, h = _proj_residual(a, bf(ret_w_out[j]), xt, g_ffn, tm=512, nk=2)
        a = _swiglu(h, ffn_w_in, i, tm=2048, tn=512)
        if i + 1 < depth:
            xt, h = _proj_residual(a, bf(ffn_w_out[i]), xt, norm_mix[i + 1][None, :],
                                   tm=512, nk=2)
        else:
            out = _proj_residual(a, bf(ffn_w_out[i]), xt, norm_final[None, :],
                                 tm=512, nk=2, final=True)
    return out.reshape(batch, seq, d)
```

```python
import functools

import jax
import jax.numpy as jnp
from jax import lax
from jax.experimental import pallas as pl
from jax.experimental.pallas import tpu as pltpu

F32 = jnp.float32
BF16 = jnp.bfloat16

D_MODEL = 2048
NORM_EPS = 1e-6
CONV_WIDTH = 3
GLA_HEADS = 4
GLA_DK = 256
GLA_DV = 512
GLA_KEY_DIM = GLA_HEADS * GLA_DK
GLA_GATE_RANK = 16
GLA_GATE_TAU = 16.0
GLA_CHUNK = 64
RET_HEADS = 8
RET_DK = 256
RET_DV = 512
RET_V_WIDTH = RET_HEADS * RET_DV
RET_CHUNK = 128
ROPE_BASE = 10000.0

LANES = 128
SUBLANES = 8
VMEM_LIMIT = 56 * 1024 * 1024
SUB_ROWS = 512
CORE_HEADS = 2

NT_DIMS = (((1,), (1,)), ((), ()))
TN_DIMS = (((0,), (0,)), ((), ()))


def _params(semantics):
    return pltpu.CompilerParams(dimension_semantics=semantics,
                                vmem_limit_bytes=VMEM_LIMIT)


def _silu(v):
    return v * (1.0 / (1.0 + jnp.exp(-v)))


def _log_sigmoid(v):
    return jnp.minimum(v, 0.0) - jnp.log1p(jnp.exp(-jnp.abs(v)))


def _rms_rows(x, eps=NORM_EPS):
    return x * lax.rsqrt(jnp.mean(x * x, axis=-1, keepdims=True) + eps)


def _dot(a, b):
    return jnp.dot(a, b, preferred_element_type=F32)


def _split_bf16(v):
    hi = v.astype(BF16)
    lo = (v - hi.astype(F32)).astype(BF16)
    return hi, lo


def _row_blocks(total, sub):
    return [slice(r, r + sub) for r in range(0, total, sub)]


def _norm_kernel(x_ref, g_ref, o_ref):
    o_ref[...] = (_rms_rows(x_ref[...]) * g_ref[...]).astype(o_ref.dtype)


def _norm(x, g, *, tm, out_dtype):
    t, d = x.shape
    return pl.pallas_call(
        _norm_kernel,
        grid=(t // tm,),
        in_specs=[pl.BlockSpec((tm, d), lambda m: (m, 0)),
                  pl.BlockSpec((1, d), lambda m: (0, 0))],
        out_specs=pl.BlockSpec((tm, d), lambda m: (m, 0)),
        out_shape=jax.ShapeDtypeStruct((t, d), out_dtype),
        compiler_params=_params(("parallel",)),
        name="norm",
    )(x, g)


def _proj_kernel(h_ref, w_ref, o_ref):
    w = w_ref[...].astype(BF16)
    for rows in _row_blocks(o_ref.shape[0], SUB_ROWS):
        o_ref[rows, :] = _dot(h_ref[rows, :], w).astype(o_ref.dtype)


def _gla_proj_kernel(h_ref, w_ref, cs_ref, wz_ref, wg_ref, bg_ref,
                     o_ref, ahi_ref, alo_ref):
    w = w_ref[...].astype(BF16)
    cs = cs_ref[...]
    for rows in _row_blocks(o_ref.shape[0], SUB_ROWS):
        o_ref[rows, :] = (_dot(h_ref[rows, :], w) * cs).astype(o_ref.dtype)

    gr = ahi_ref.shape[0]
    grows = pl.ds(pl.multiple_of(pl.program_id(1) * gr, gr), gr)
    z = _dot(h_ref[grows, :], wz_ref[...].astype(BF16))
    z_hi, z_lo = _split_bf16(z)
    wg = wg_ref[...].astype(BF16)
    pre = _dot(jnp.concatenate([z_hi, z_lo], axis=1),
               jnp.concatenate([wg, wg], axis=0)) + bg_ref[...]
    a_hi, a_lo = _split_bf16(_log_sigmoid(pre) * (1.0 / GLA_GATE_TAU))
    ahi_ref[...] = a_hi
    alo_ref[...] = a_lo


def _swiglu_kernel(h_ref, wg_ref, wu_ref, o_ref):
    wg = wg_ref[...].astype(BF16)
    wu = wu_ref[...].astype(BF16)
    for rows in _row_blocks(o_ref.shape[0], SUB_ROWS):
        h = h_ref[rows, :]
        o_ref[rows, :] = (_silu(_dot(h, wg)) * _dot(h, wu)).astype(o_ref.dtype)


def _conv_kernel(h_ref, wb_ref, wc_ref, wu_ref, cw_ref, o_ref,
                 u_ref, carry_ref, *, blocks_per_seq):
    m = pl.program_id(0)
    n = pl.program_id(1)
    tm = o_ref.shape[0]
    wb = wb_ref[...].astype(BF16)
    wc = wc_ref[...].astype(BF16)
    wu = wu_ref[...].astype(BF16)
    cw = cw_ref[...]
    first = (m % blocks_per_seq) == 0
    prev_tail = carry_ref[n]
    u_ref[0:SUBLANES, :] = jnp.where(first, jnp.zeros_like(prev_tail), prev_tail)
    for rows in _row_blocks(tm, SUB_ROWS):
        h = h_ref[rows, :]
        u = _dot(h, wc) * _dot(h, wu)
        lo, hi = rows.start + SUBLANES, rows.stop + SUBLANES
        u_ref[lo:hi, :] = u
        conv = (cw[0:1, :] * u_ref[lo - 2:hi - 2, :]
                + cw[1:2, :] * u_ref[lo - 1:hi - 1, :]
                + cw[2:3, :] * u)
        o_ref[rows, :] = (_dot(h, wb) * conv).astype(o_ref.dtype)
    carry_ref[n] = u_ref[tm:tm + SUBLANES, :]


def _row_spec(tm, width):
    return pl.BlockSpec((tm, width), lambda m, n: (m, 0))


def _col_spec(layer, k, tn, offset_blocks=0):
    return pl.BlockSpec((None, k, tn), lambda m, n: (layer, 0, n + offset_blocks))


def _proj(h, w, layer, *, n_out, tm, tn):
    t, d = h.shape
    return pl.pallas_call(
        _proj_kernel,
        grid=(t // tm, n_out // tn),
        in_specs=[_row_spec(tm, d), _col_spec(layer, d, tn)],
        out_specs=pl.BlockSpec((tm, tn), lambda m, n: (m, n)),
        out_shape=jax.ShapeDtypeStruct((t, n_out), BF16),
        compiler_params=_params(("parallel", "arbitrary")),
        name="proj",
    )(h, w)


def _gla_proj(h, w, layer, col_scale, wz, w_gate2, b_gate, *, n_out, tm, tn):
    t, d = h.shape
    nb = n_out // tn
    gr = tm // nb
    zw = wz.shape[1]
    gate_spec = pl.BlockSpec((gr, GLA_KEY_DIM), lambda m, n: (m * nb + n, 0))
    const = lambda shape: pl.BlockSpec(shape, lambda m, n: (0, 0))
    return pl.pallas_call(
        _gla_proj_kernel,
        grid=(t // tm, nb),
        in_specs=[_row_spec(tm, d), _col_spec(layer, d, tn),
                  pl.BlockSpec((1, tn), lambda m, n: (0, n)),
                  const((d, zw)), const((zw, GLA_KEY_DIM)), const((1, GLA_KEY_DIM))],
        out_specs=[pl.BlockSpec((tm, tn), lambda m, n: (m, n)), gate_spec, gate_spec],
        out_shape=[jax.ShapeDtypeStruct((t, n_out), BF16),
                   jax.ShapeDtypeStruct((t, GLA_KEY_DIM), BF16),
                   jax.ShapeDtypeStruct((t, GLA_KEY_DIM), BF16)],
        compiler_params=_params(("parallel", "arbitrary")),
        name="gla_proj",
    )(h, w, col_scale, wz, w_gate2, b_gate)


def _swiglu(h, w_in, layer, *, tm, tn):
    t, d = h.shape
    d_ff = w_in.shape[2] // 2
    return pl.pallas_call(
        _swiglu_kernel,
        grid=(t // tm, d_ff // tn),
        in_specs=[_row_spec(tm, d), _col_spec(layer, d, tn),
                  _col_spec(layer, d, tn, d_ff // tn)],
        out_specs=pl.BlockSpec((tm, tn), lambda m, n: (m, n)),
        out_shape=jax.ShapeDtypeStruct((t, d_ff), BF16),
        compiler_params=_params(("parallel", "arbitrary")),
        name="swiglu",
    )(h, w_in, w_in)


def _conv(h, w_in, conv_w, layer, *, seq, tm, tn):
    t, d = h.shape
    nb = d // tn
    kern = functools.partial(_conv_kernel, blocks_per_seq=seq // tm)
    return pl.pallas_call(
        kern,
        grid=(t // tm, nb),
        in_specs=[_row_spec(tm, d),
                  _col_spec(layer, d, tn), _col_spec(layer, d, tn, nb),
                  _col_spec(layer, d, tn, 2 * nb),
                  pl.BlockSpec((None, CONV_WIDTH, tn), lambda m, n: (layer, 0, n))],
        out_specs=pl.BlockSpec((tm, tn), lambda m, n: (m, n)),
        out_shape=jax.ShapeDtypeStruct((t, d), BF16),
        scratch_shapes=[pltpu.VMEM((tm + SUBLANES, tn), F32),
                        pltpu.VMEM((nb, SUBLANES, tn), F32)],
        compiler_params=_params(("arbitrary", "arbitrary")),
        name="conv",
    )(h, w_in, w_in, w_in, conv_w)


def _proj_residual_kernel(*refs, gated, head_gain, final, sub_rows):
    refs = list(refs)
    a_ref = refs.pop(0)
    gate_ref = refs.pop(0) if gated else None
    gain_ref = refs.pop(0) if head_gain else None
    w_ref, x_ref, g_ref = refs[:3]
    out_refs = refs[3:]
    g = g_ref[...]
    for rows in _row_blocks(x_ref.shape[0], sub_rows):
        a = a_ref[rows, :]
        if gated:
            o = a.astype(F32)
            gate = gate_ref[rows, :].astype(F32)
            heads = []
            for cols in _row_blocks(a.shape[1], RET_DV):
                oh = _rms_rows(o[:, cols])
                if head_gain:
                    oh = oh * gain_ref[...]
                heads.append((_silu(gate[:, cols]) * oh).astype(BF16))
            a = jnp.concatenate(heads, axis=1)
        xn = x_ref[rows, :] + _dot(a, w_ref[...])
        normed = _rms_rows(xn) * g
        if final:
            out_refs[0][rows, :] = normed
        else:
            out_refs[0][rows, :] = xn
            out_refs[1][rows, :] = normed.astype(BF16)


def _proj_residual(a, w, x, g, *, tm, sub_rows, gate=None, gate_block=0,
                   head_gain=None, final=False):
    t, kdim = a.shape
    d = w.shape[1]
    row = lambda width, blk=0: pl.BlockSpec((tm, width), lambda m: (m, blk))
    const = lambda shape: pl.BlockSpec(shape, lambda m: (0, 0))
    operands, in_specs = [a], [row(kdim)]
    if gate is not None:
        operands.append(gate)
        in_specs.append(row(kdim, gate_block))
    if head_gain is not None:
        operands.append(head_gain)
        in_specs.append(const(head_gain.shape))
    operands += [w, x, g]
    in_specs += [pl.BlockSpec((kdim, d), lambda m: (0, 0),
                              pipeline_mode=pl.Buffered(1)),
                 row(d), const((1, d))]
    if final:
        out_specs, out_shape = row(d), jax.ShapeDtypeStruct((t, d), F32)
    else:
        out_specs = [row(d), row(d)]
        out_shape = [jax.ShapeDtypeStruct((t, d), F32),
                     jax.ShapeDtypeStruct((t, d), BF16)]
    kern = functools.partial(_proj_residual_kernel, gated=gate is not None,
                             head_gain=head_gain is not None, final=final,
                             sub_rows=sub_rows)
    return pl.pallas_call(
        kern,
        grid=(t // tm,),
        in_specs=in_specs,
        out_specs=out_specs,
        out_shape=out_shape,
        compiler_params=_params(("parallel",)),
        name="proj_residual",
    )(*operands)


def _gla_kernel(q_ref, k_ref, v_ref, ahi_ref, alo_ref, o_ref, st_ref,
                cum_ref, last_ref, qt_ref, kt_ref, ke_ref, oi_ref, kv_ref, att_ref,
                *, chunks):
    c = GLA_CHUNK

    @pl.when(pl.program_id(2) == 0)
    def _():
        st_ref[...] = jnp.zeros_like(st_ref)

    row = lax.broadcasted_iota(jnp.int32, (c, c), 0)
    col = lax.broadcasted_iota(jnp.int32, (c, c), 1)
    causal = row >= col
    tril = jnp.where(causal, 1.0, 0.0).astype(BF16)

    chunk_rows = _row_blocks(chunks * c, c)
    heads = [(slice(p * GLA_DK, (p + 1) * GLA_DK), slice(p * GLA_DV, (p + 1) * GLA_DV))
             for p in range(CORE_HEADS)]

    for rows in chunk_rows:
        cum = _dot(tril, ahi_ref[rows, :]) + _dot(tril, alo_ref[rows, :])
        cum_ref[rows, :] = cum
        last_ref[rows, :] = jnp.broadcast_to(cum[c - 1:c, :], cum.shape)
    cum = cum_ref[...]
    last = last_ref[...]
    k = k_ref[...].astype(F32)
    qt_ref[...] = (q_ref[...].astype(F32) * jnp.exp(cum)).astype(BF16)
    kt_ref[...] = (k * jnp.exp(-cum)).astype(BF16)
    ke_ref[...] = (k * jnp.exp(last - cum)).astype(BF16)

    for i, rows in enumerate(chunk_rows):
        for p, (kc, vc) in enumerate(heads):
            att = lax.dot_general(qt_ref[rows, kc], kt_ref[rows, kc], NT_DIMS,
                                  preferred_element_type=F32)
            att_ref[rows, p * c:(p + 1) * c] = jnp.where(causal, att, 0.0).astype(BF16)
    for i, rows in enumerate(chunk_rows):
        for p, (kc, vc) in enumerate(heads):
            kv_ref[i, p] = lax.dot_general(v_ref[rows, vc], ke_ref[rows, kc], TN_DIMS,
                                           preferred_element_type=F32)
    for i, rows in enumerate(chunk_rows):
        for p, (kc, vc) in enumerate(heads):
            oi_ref[rows, vc] = _dot(att_ref[rows, p * c:(p + 1) * c], v_ref[rows, vc])

    for p, (kc, vc) in enumerate(heads):
        st = st_ref[p]
        for i, rows in enumerate(chunk_rows):
            o = oi_ref[rows, vc] + lax.dot_general(
                qt_ref[rows, kc], st.astype(BF16), NT_DIMS, preferred_element_type=F32)
            o_ref[rows, vc] = o.astype(o_ref.dtype)
            decay = jnp.exp(last_ref[rows.start:rows.start + 1, kc])
            st = st * decay + kv_ref[i, p]
        st_ref[p] = st


def _gla_core(proj, a_hi, a_lo, *, batch, seq, tr):
    t = proj.shape[0]
    nblk = seq // tr
    kw, vw = CORE_HEADS * GLA_DK, CORE_HEADS * GLA_DV
    k_off = GLA_KEY_DIM // kw
    v_off = 2 * GLA_KEY_DIM // vw
    rows = lambda b, h, s: b * nblk + s
    kern = functools.partial(_gla_kernel, chunks=tr // GLA_CHUNK)
    return pl.pallas_call(
        kern,
        grid=(batch, GLA_HEADS // CORE_HEADS, nblk),
        in_specs=[
            pl.BlockSpec((tr, kw), lambda b, h, s: (rows(b, h, s), h)),
            pl.BlockSpec((tr, kw), lambda b, h, s: (rows(b, h, s), k_off + h)),
            pl.BlockSpec((tr, vw), lambda b, h, s: (rows(b, h, s), v_off + h)),
            pl.BlockSpec((tr, kw), lambda b, h, s: (rows(b, h, s), h)),
            pl.BlockSpec((tr, kw), lambda b, h, s: (rows(b, h, s), h)),
        ],
        out_specs=pl.BlockSpec((tr, vw), lambda b, h, s: (rows(b, h, s), h)),
        out_shape=jax.ShapeDtypeStruct((t, D_MODEL), BF16),
        scratch_shapes=[pltpu.VMEM((CORE_HEADS, GLA_DV, GLA_DK), F32),
                        pltpu.VMEM((tr, kw), F32), pltpu.VMEM((tr, kw), F32),
                        pltpu.VMEM((tr, kw), BF16), pltpu.VMEM((tr, kw), BF16),
                        pltpu.VMEM((tr, kw), BF16), pltpu.VMEM((tr, vw), F32),
                        pltpu.VMEM((tr // GLA_CHUNK, CORE_HEADS, GLA_DV, GLA_DK), F32),
                        pltpu.VMEM((tr, CORE_HEADS * GLA_CHUNK), BF16)],
        compiler_params=_params(("parallel", "parallel", "arbitrary")),
        name="gla_core",
    )(proj, proj, proj, a_hi, a_lo)


def _ret_kernel(q_ref, k_ref, v_ref, cos_ref, sin_ref, dmat_ref,
                qf_ref, kf_ref, dec_ref, o_ref, s_ref,
                qr_ref, kr_ref, qi_ref, ki_ref, sc_ref, oi_ref, kv_ref, *, chunks):
    c = RET_CHUNK

    @pl.when(pl.program_id(2) == 0)
    def _():
        s_ref[...] = jnp.zeros_like(s_ref)

    def rotary(t, cos, sin_signed):
        return t * cos + pltpu.roll(t, RET_DK // 2, axis=1) * sin_signed

    chunk_rows = _row_blocks(chunks * c, c)
    heads = [(slice(p * RET_DK, (p + 1) * RET_DK), slice(p * RET_DV, (p + 1) * RET_DV))
             for p in range(CORE_HEADS)]

    for rows in chunk_rows:
        cos = cos_ref[rows, :]
        sin = sin_ref[rows, :]
        for p, (kc, vc) in enumerate(heads):
            q = rotary(q_ref[rows, kc].astype(F32), cos, sin)
            k = rotary(k_ref[rows, kc].astype(F32), cos, sin) * (RET_DK ** -0.5)
            qr_ref[rows, kc] = q.astype(BF16)
            kr_ref[rows, kc] = k.astype(BF16)
            qi_ref[rows, kc] = (q * qf_ref[p]).astype(BF16)
            ki_ref[rows, kc] = (k * kf_ref[p]).astype(BF16)

    for rows in chunk_rows:
        for p, (kc, vc) in enumerate(heads):
            scores = lax.dot_general(qr_ref[rows, kc], kr_ref[rows, kc], NT_DIMS,
                                     preferred_element_type=F32) * dmat_ref[p]
            sc_ref[rows, p * c:(p + 1) * c] = scores.astype(BF16)
    for i, rows in enumerate(chunk_rows):
        for p, (kc, vc) in enumerate(heads):
            kv_ref[i, p] = lax.dot_general(ki_ref[rows, kc], v_ref[rows, vc], TN_DIMS,
                                           preferred_element_type=F32)
    for rows in chunk_rows:
        for p, (kc, vc) in enumerate(heads):
            oi_ref[rows, vc] = _dot(sc_ref[rows, p * c:(p + 1) * c], v_ref[rows, vc])

    for p, (kc, vc) in enumerate(heads):
        s = s_ref[p]
        for i, rows in enumerate(chunk_rows):
            o = oi_ref[rows, vc] + _dot(qi_ref[rows, kc], s.astype(BF16))
            o_ref[rows, vc] = o.astype(o_ref.dtype)
            s = s * dec_ref[p] + kv_ref[i, p]
        s_ref[p] = s


def _ret_tables(seq):
    pos = jnp.arange(seq, dtype=F32)
    inv_freq = 1.0 / (ROPE_BASE ** jnp.linspace(0.0, 1.0, RET_DK // 2, dtype=F32))
    ang = pos[:, None] * inv_freq[None, :]
    cos = jnp.cos(jnp.concatenate([ang, ang], axis=-1))
    sin = jnp.sin(ang)
    sin_signed = jnp.concatenate([-sin, sin], axis=-1)
    log_gamma = jnp.log(1.0 - 2.0 ** (-5.0 - jnp.arange(RET_HEADS, dtype=F32)))
    idx = jnp.arange(RET_CHUNK, dtype=F32)
    rel = idx[:, None] - idx[None, :]
    dmat = jnp.where(rel >= 0,
                     jnp.exp(jnp.maximum(rel, 0.0)[None] * log_gamma[:, None, None]), 0.0)
    ones = jnp.ones((1, 1, RET_DK), F32)
    qfac = jnp.exp(log_gamma[:, None, None] * (idx + 1.0)[None, :, None]) * ones
    kfac = jnp.exp(log_gamma[:, None, None] * (RET_CHUNK - 1.0 - idx)[None, :, None]) * ones
    decay = jnp.exp(log_gamma * RET_CHUNK)[:, None, None] * jnp.ones((1, 1, RET_DV), F32)
    return cos, sin_signed, dmat, qfac, kfac, decay


def _ret_core(proj, *, batch, seq, tr):
    t = proj.shape[0]
    nblk = seq // tr
    cos, sin_signed, dmat, qfac, kfac, decay = _ret_tables(seq)
    kw, vw = CORE_HEADS * RET_DK, CORE_HEADS * RET_DV
    k_off = D_MODEL // kw
    v_off = 2 * D_MODEL // vw
    rows = lambda b, h, s: b * nblk + s
    kern = functools.partial(_ret_kernel, chunks=tr // RET_CHUNK)
    head_tab = lambda r, w: pl.BlockSpec((CORE_HEADS, r, w), lambda b, h, s: (h, 0, 0))
    return pl.pallas_call(
        kern,
        grid=(batch, RET_HEADS // CORE_HEADS, nblk),
        in_specs=[
            pl.BlockSpec((tr, kw), lambda b, h, s: (rows(b, h, s), h)),
            pl.BlockSpec((tr, kw), lambda b, h, s: (rows(b, h, s), k_off + h)),
            pl.BlockSpec((tr, vw), lambda b, h, s: (rows(b, h, s), v_off + h)),
            pl.BlockSpec((tr, RET_DK), lambda b, h, s: (s, 0)),
            pl.BlockSpec((tr, RET_DK), lambda b, h, s: (s, 0)),
            head_tab(RET_CHUNK, RET_CHUNK),
            head_tab(RET_CHUNK, RET_DK),
            head_tab(RET_CHUNK, RET_DK),
            head_tab(1, RET_DV),
        ],
        out_specs=pl.BlockSpec((tr, vw), lambda b, h, s: (rows(b, h, s), h)),
        out_shape=jax.ShapeDtypeStruct((t, RET_V_WIDTH), BF16),
        scratch_shapes=[pltpu.VMEM((CORE_HEADS, RET_DK, RET_DV), F32),
                        pltpu.VMEM((tr, kw), BF16), pltpu.VMEM((tr, kw), BF16),
                        pltpu.VMEM((tr, kw), BF16), pltpu.VMEM((tr, kw), BF16),
                        pltpu.VMEM((tr, CORE_HEADS * RET_CHUNK), BF16),
                        pltpu.VMEM((tr, vw), F32),
                        pltpu.VMEM((tr // RET_CHUNK, CORE_HEADS, RET_DK, RET_DV), F32)],
        compiler_params=_params(("parallel", "parallel", "arbitrary")),
        name="ret_core",
    )(proj, proj, proj, cos, sin_signed, dmat, qfac, kfac, decay)


def kernel(x, norm_mix, norm_ffn, norm_final, conv_w_in, conv_w, conv_w_out,
           gla_w_in, gla_w_gate2, gla_b_gate, gla_norm, gla_w_out,
           ret_w_in, ret_w_out, ffn_w_in, ffn_w_out):
    batch, seq, d = x.shape
    depth = norm_mix.shape[0]
    xt = x.reshape(batch * seq, d)
    bf = lambda w: w.astype(BF16)
    gla_main = 2 * GLA_KEY_DIM + 2 * D_MODEL
    pad_rank = LANES - GLA_GATE_RANK
    gla_col_scale = jnp.where(jnp.arange(gla_main) < GLA_KEY_DIM,
                              GLA_DK ** -0.5, 1.0).astype(F32)[None, :]

    h = _norm(xt, norm_mix[0][None, :], tm=512, out_dtype=BF16)
    for i in range(depth):
        kind, j = i % 3, i // 3
        g_ffn = norm_ffn[i][None, :]
        if kind == 0:
            a = _conv(h, conv_w_in, conv_w, j, seq=seq, tm=2048, tn=256)
            xt, h = _proj_residual(a, bf(conv_w_out[j]), xt, g_ffn, tm=512, sub_rows=256)
        elif kind == 1:
            wz = jnp.pad(gla_w_in[j, :, gla_main:], ((0, 0), (0, pad_rank)))
            wg2 = jnp.pad(gla_w_gate2[j], ((0, pad_rank), (0, 0)))
            proj, a_hi, a_lo = _gla_proj(h, gla_w_in, j, gla_col_scale, wz, wg2,
                                         gla_b_gate[j][None, :],
                                         n_out=gla_main, tm=2048, tn=768)
            o = _gla_core(proj, a_hi, a_lo, batch=batch, seq=seq, tr=512)
            xt, h = _proj_residual(o, bf(gla_w_out[j]), xt, g_ffn, tm=512, sub_rows=256,
                                   gate=proj, gate_block=2,
                                   head_gain=gla_norm[j][None, :])
        else:
            proj = _proj(h, ret_w_in, j, n_out=ret_w_in.shape[2], tm=2048, tn=512)
            o = _ret_core(proj, batch=batch, seq=seq, tr=512)
            xt, h = _proj_residual(o, bf(ret_w_out[j]), xt, g_ffn, tm=256, sub_rows=128,
                                   gate=proj, gate_block=2)
        a = _swiglu(h, ffn_w_in, i, tm=2048, tn=512)
        if i + 1 < depth:
            xt, h = _proj_residual(a, bf(ffn_w_out[i]), xt, norm_mix[i + 1][None, :],
                                   tm=256, sub_rows=128)
        else:
            out = _proj_residual(a, bf(ffn_w_out[i]), xt, norm_final[None, :],
                                 tm=256, sub_rows=128, final=True)
    return out.reshape(batch, seq, d)
```
